```python
import math
import jax, jax.numpy as jnp
from jax import lax
import numpy as np

D_MODEL = 1024
BATCH = 8
SEQ = 2048
DEPTH = 2

N_MIXERS = 2
HEAD_DIM = 64
MIX_WIDTH = D_MODEL
MEM_HEADS = 4
MEM_WIDTH = MEM_HEADS * HEAD_DIM
MEM_TOKENS = 256
TOK_WIDTH = MIX_WIDTH - MEM_WIDTH
HYENA_CH = TOK_WIDTH
HYENA_IN = 3 * HYENA_CH
HYENA_EMB = 33
HYENA_FILTER_W = 64
DECAY_TARGET = 1e-2
SHORTEST_DECAY_FRAC = 0.3
LONGEST_DECAY_FRAC = 1.5
SHORT_CONV = 3
SWA_HEADS = TOK_WIDTH // HEAD_DIM
SWA_KV_HEADS = SWA_HEADS // 3
SWA_IN = (SWA_HEADS + 2 * SWA_KV_HEADS) * HEAD_DIM
WINDOW = 128
BLOCK = 128
ROPE_THETA = 10000.0
D_FF = 128 * int(math.ceil(8 * D_MODEL / 3 / 128))
FFN_CONV = 3
LN_EPS = 1e-5
DEEPNORM_ALPHA = (2 * DEPTH) ** 0.25
DEEPNORM_BETA = (8 * DEPTH) ** -0.25
NEG_INF = -1e30

kernel_name = "hybrid_hyena_swa_memory_encoder"


def layer_norm(x, g, b):
    xf = x.astype(jnp.float32)
    mu = jnp.mean(xf, axis=-1, keepdims=True)
    var = jnp.mean(jnp.square(xf - mu), axis=-1, keepdims=True)
    y = (xf - mu) * lax.rsqrt(var + LN_EPS) * g.astype(jnp.float32) + b.astype(jnp.float32)
    return y.astype(x.dtype)


def dwconv3(x, w, b):
    xp = jnp.pad(x, ((0, 0), (1, 1), (0, 0)))
    return xp[:, :-2] * w[0] + xp[:, 1:-1] * w[1] + xp[:, 2:] * w[2] + b


def rope_tables(L):
    inv = ROPE_THETA ** (-jnp.arange(0, HEAD_DIM, 2, dtype=jnp.float32) / HEAD_DIM)
    ang = jnp.arange(L, dtype=jnp.float32)[:, None] * inv[None, :]
    ang = jnp.concatenate([ang, ang], axis=-1)
    return jnp.cos(ang), jnp.sin(ang)


def apply_rope(x, cos, sin):
    xf = x.astype(jnp.float32)
    x1, x2 = jnp.split(xf, 2, axis=-1)
    rot = jnp.concatenate([-x2, x1], axis=-1)
    return (xf * cos[None, :, None, :] + rot * sin[None, :, None, :]).astype(x.dtype)


def hyena_filters(L, w1, b1, f1, w2, b2, f2, w3, b3, f3, w_out):
    f32 = jnp.float32
    t = jnp.linspace(0.0, 1.0, L, dtype=f32)[:, None]
    bands = (HYENA_EMB - 1) // 2
    w = 2.0 * math.pi * jnp.arange(L, dtype=f32)[:, None] / L
    fr = jnp.linspace(1e-4, bands - 1, bands, dtype=f32)[None, :]
    z = jnp.concatenate([t, jnp.cos(fr * w), -jnp.sin(fr * w)], axis=-1)
    h = jnp.sin(f1.astype(f32) * (z @ w1.astype(f32) + b1.astype(f32)))
    h = jnp.sin(f2.astype(f32) * (h @ w2.astype(f32) + b2.astype(f32)))
    h = jnp.sin(f3.astype(f32) * (h @ w3.astype(f32) + b3.astype(f32)))
    h = h @ w_out.astype(f32)
    min_decay = math.log(DECAY_TARGET) / LONGEST_DECAY_FRAC
    max_decay = math.log(DECAY_TARGET) / SHORTEST_DECAY_FRAC
    deltas = jnp.abs(jnp.linspace(min_decay, max_decay, HYENA_CH, dtype=f32))
    decay = jnp.exp(-t * deltas[None, :])
    return h[:, :HYENA_CH] * decay, h[:, HYENA_CH:] * decay


def bidirectional_long_conv(u, h_fwd, h_bwd):
    L, C = h_fwd.shape
    k0 = h_fwd.at[0].add(h_bwd[0])
    k_circ = jnp.concatenate([k0, jnp.zeros((1, C), jnp.float32), h_bwd[1:][::-1]], axis=0)
    U = jnp.fft.rfft(u, n=2 * L, axis=1)
    K = jnp.fft.rfft(k_circ, n=2 * L, axis=0)
    return jnp.fft.irfft(U * K[None], n=2 * L, axis=1)[:, :L]


def hyena_mixer(u, conv_w, conv_b, w1, b1, f1, w2, b2, f2, w3, b3, f3, filt_w_out, d_bias):
    L = u.shape[1]
    uc = dwconv3(u, conv_w, conv_b)
    x0, x1, v = jnp.split(uc, 3, axis=-1)
    h_fwd, h_bwd = hyena_filters(L, w1, b1, f1, w2, b2, f2, w3, b3, f3, filt_w_out)
    z = (v * x1).astype(jnp.float32)
    z = bidirectional_long_conv(z, h_fwd, h_bwd) + z * d_bias.astype(jnp.float32)
    return z.astype(u.dtype) * x0


def windowed_gqa_sink(q, k, v, sink):
    B, L, H, hd = q.shape
    kvh = k.shape[2]
    g = H // kvh
    nb = L // BLOCK
    qb = q.reshape(B, nb, BLOCK, kvh, g, hd)
    pad = ((0, 0), (BLOCK, BLOCK), (0, 0), (0, 0))
    kp = jnp.pad(k, pad).reshape(B, nb + 2, BLOCK, kvh, hd)
    vp = jnp.pad(v, pad).reshape(B, nb + 2, BLOCK, kvh, hd)
    kw = jnp.concatenate([kp[:, :-2], kp[:, 1:-1], kp[:, 2:]], axis=2)
    vw = jnp.concatenate([vp[:, :-2], vp[:, 1:-1], vp[:, 2:]], axis=2)
    s = jnp.einsum('bnqkgd,bnskd->bnkgqs', qb, kw).astype(jnp.float32) * (hd ** -0.5)
    blk = jnp.arange(nb)[:, None]
    qpos = blk * BLOCK + jnp.arange(BLOCK)[None, :]
    kpos = (blk - 1) * BLOCK + jnp.arange(3 * BLOCK)[None, :]
    rel = kpos[:, None, :] - qpos[:, :, None]
    valid = (jnp.abs(rel) <= WINDOW) & (kpos[:, None, :] >= 0) & (kpos[:, None, :] < L)
    s = jnp.where(valid[None, :, None, None], s, NEG_INF)
    sink_logit = jnp.broadcast_to(
        sink.astype(jnp.float32).reshape(kvh, g)[None, None, :, :, None, None],
        s.shape[:-1] + (1,))
    p = jax.nn.softmax(jnp.concatenate([s, sink_logit], axis=-1), axis=-1)[..., :-1]
    o = jnp.einsum('bnkgqs,bnskd->bnqkgd', p.astype(v.dtype), vw)
    return o.reshape(B, L, H * hd)


def swa_mixer(u, sink, cos, sin):
    B, L, _ = u.shape
    q, k, v = jnp.split(u, [SWA_HEADS * HEAD_DIM, (SWA_HEADS + SWA_KV_HEADS) * HEAD_DIM], axis=-1)
    q = apply_rope(q.reshape(B, L, SWA_HEADS, HEAD_DIM), cos, sin)
    k = apply_rope(k.reshape(B, L, SWA_KV_HEADS, HEAD_DIM), cos, sin)
    v = v.reshape(B, L, SWA_KV_HEADS, HEAD_DIM)
    return windowed_gqa_sink(q, k, v, sink)


def memory_attention(mq, mem_k, mem_v):
    B, L, _ = mq.shape
    q = mq.reshape(B, L, MEM_HEADS, HEAD_DIM)
    s = jnp.einsum('blhd,bmhd->bhlm', q, mem_k).astype(jnp.float32) * (HEAD_DIM ** -0.5)
    p = jax.nn.softmax(s, axis=-1).astype(mem_v.dtype)
    o = jnp.einsum('bhlm,bmhd->blhd', p, mem_v)
    return o.reshape(B, L, MEM_WIDTH)


def conv_glu_ffn(x, w_up, conv_w, conv_b, w_down):
    h = dwconv3(x @ w_up, conv_w, conv_b)
    a, gate = jnp.split(h, 2, axis=-1)
    return (jax.nn.silu(gate) * a) @ w_down


def _normal(k, shape, scale):
    return jax.random.normal(k, shape, jnp.float32) * scale


def setup_inputs(seed: int = 0) -> dict:
    key = jax.random.key(seed)
    ks = iter(jax.random.split(key, 64))
    d = D_MODEL
    W = HYENA_FILTER_W
    beta = DEEPNORM_BETA
    inp = {}
    inp['x'] = _normal(next(ks), (BATCH, SEQ, d), 1.0)
    inp['mem'] = _normal(next(ks), (BATCH, MEM_TOKENS, d), 1.0)
    inp['w_mem_kv'] = _normal(next(ks), (d, 2 * MEM_WIDTH), d ** -0.5)
    inp['l0_w_in'] = _normal(next(ks), (d, HYENA_IN + MEM_WIDTH), d ** -0.5)
    inp['l0_conv_w'] = _normal(next(ks), (SHORT_CONV, HYENA_IN), SHORT_CONV ** -0.5)
    inp['l0_conv_b'] = _normal(next(ks), (HYENA_IN,), 0.02)
    inp['l0_filt_w1'] = _normal(next(ks), (HYENA_EMB, W), HYENA_EMB ** -0.5)
    inp['l0_filt_b1'] = _normal(next(ks), (W,), 0.1)
    inp['l0_filt_f1'] = 1.0 + _normal(next(ks), (W,), 0.01)
    inp['l0_filt_w2'] = _normal(next(ks), (W, W), W ** -0.5)
    inp['l0_filt_b2'] = _normal(next(ks), (W,), 0.1)
    inp['l0_filt_f2'] = 1.0 + _normal(next(ks), (W,), 0.01)
    inp['l0_filt_w3'] = _normal(next(ks), (W, W), W ** -0.5)
    inp['l0_filt_b3'] = _normal(next(ks), (W,), 0.1)
    inp['l0_filt_f3'] = 1.0 + _normal(next(ks), (W,), 0.01)
    inp['l0_filt_w_out'] = _normal(next(ks), (W, 2 * HYENA_CH), 0.1 * W ** -0.5)
    inp['l0_hyena_d'] = _normal(next(ks), (HYENA_CH,), 0.5)
    inp['l0_w_out'] = _normal(next(ks), (MIX_WIDTH, d), beta * MIX_WIDTH ** -0.5)
    inp['l0_ln1_g'] = 1.0 + _normal(next(ks), (d,), 0.02)
    inp['l0_ln1_b'] = _normal(next(ks), (d,), 0.02)
    inp['l0_ffn_w_up'] = _normal(next(ks), (d, 2 * D_FF), d ** -0.5)
    inp['l0_ffn_conv_w'] = _normal(next(ks), (FFN_CONV, 2 * D_FF), FFN_CONV ** -0.5)
    inp['l0_ffn_conv_b'] = _normal(next(ks), (2 * D_FF,), 0.02)
    inp['l0_ffn_w_down'] = _normal(next(ks), (D_FF, d), beta * D_FF ** -0.5)
    inp['l0_ln2_g'] = 1.0 + _normal(next(ks), (d,), 0.02)
    inp['l0_ln2_b'] = _normal(next(ks), (d,), 0.02)
    inp['l1_w_in'] = _normal(next(ks), (d, SWA_IN + MEM_WIDTH), d ** -0.5)
    inp['l1_sink'] = _normal(next(ks), (SWA_HEADS,), 0.5)
    inp['l1_w_out'] = _normal(next(ks), (MIX_WIDTH, d), beta * MIX_WIDTH ** -0.5)
    inp['l1_ln1_g'] = 1.0 + _normal(next(ks), (d,), 0.02)
    inp['l1_ln1_b'] = _normal(next(ks), (d,), 0.02)
    inp['l1_ffn_w_up'] = _normal(next(ks), (d, 2 * D_FF), d ** -0.5)
    inp['l1_ffn_conv_w'] = _normal(next(ks), (FFN_CONV, 2 * D_FF), FFN_CONV ** -0.5)
    inp['l1_ffn_conv_b'] = _normal(next(ks), (2 * D_FF,), 0.02)
    inp['l1_ffn_w_down'] = _normal(next(ks), (D_FF, d), beta * D_FF ** -0.5)
    inp['l1_ln2_g'] = 1.0 + _normal(next(ks), (d,), 0.02)
    inp['l1_ln2_b'] = _normal(next(ks), (d,), 0.02)
    return inp


def reference(x, mem, w_mem_kv,
              l0_w_in, l0_conv_w, l0_conv_b,
              l0_filt_w1, l0_filt_b1, l0_filt_f1, l0_filt_w2, l0_filt_b2, l0_filt_f2,
              l0_filt_w3, l0_filt_b3, l0_filt_f3, l0_filt_w_out, l0_hyena_d,
              l0_w_out, l0_ln1_g, l0_ln1_b,
              l0_ffn_w_up, l0_ffn_conv_w, l0_ffn_conv_b, l0_ffn_w_down, l0_ln2_g, l0_ln2_b,
              l1_w_in, l1_sink, l1_w_out, l1_ln1_g, l1_ln1_b,
              l1_ffn_w_up, l1_ffn_conv_w, l1_ffn_conv_b, l1_ffn_w_down, l1_ln2_g, l1_ln2_b):
    B, L, _ = x.shape
    mem_kv = mem @ w_mem_kv
    mem_k, mem_v = jnp.split(mem_kv, 2, axis=-1)
    mem_k = mem_k.reshape(B, mem.shape[1], MEM_HEADS, HEAD_DIM)
    mem_v = mem_v.reshape(B, mem.shape[1], MEM_HEADS, HEAD_DIM)
    cos, sin = rope_tables(L)

    layers = [
        dict(w_in=l0_w_in,
             mix=(l0_conv_w, l0_conv_b, l0_filt_w1, l0_filt_b1, l0_filt_f1,
                  l0_filt_w2, l0_filt_b2, l0_filt_f2, l0_filt_w3, l0_filt_b3, l0_filt_f3,
                  l0_filt_w_out, l0_hyena_d),
             w_out=l0_w_out, ln1=(l0_ln1_g, l0_ln1_b),
             ffn=(l0_ffn_w_up, l0_ffn_conv_w, l0_ffn_conv_b, l0_ffn_w_down),
             ln2=(l0_ln2_g, l0_ln2_b)),
        dict(w_in=l1_w_in, mix=(l1_sink,),
             w_out=l1_w_out, ln1=(l1_ln1_g, l1_ln1_b),
             ffn=(l1_ffn_w_up, l1_ffn_conv_w, l1_ffn_conv_b, l1_ffn_w_down),
             ln2=(l1_ln2_g, l1_ln2_b)),
    ]

    for i in range(DEPTH):
        p = layers[i]
        h = x @ p['w_in']
        tok, mq = h[..., :-MEM_WIDTH], h[..., -MEM_WIDTH:]
        if i % N_MIXERS == 0:
            y_tok = hyena_mixer(tok, *p['mix'])
        else:
            y_tok = swa_mixer(tok, *p['mix'], cos, sin)
        y_mem = memory_attention(mq, mem_k, mem_v)
        y = jnp.concatenate([y_tok, y_mem], axis=-1) @ p['w_out']
        x = layer_norm(DEEPNORM_ALPHA * x + y, *p['ln1'])
        x = layer_norm(DEEPNORM_ALPHA * x + conv_glu_ffn(x, *p['ffn']), *p['ln2'])
    return x
```

```python
import functools
import math

import numpy as np
import jax
import jax.numpy as jnp
from jax import lax
from jax.experimental import pallas as pl
from jax.experimental.pallas import tpu as pltpu

f32 = jnp.float32
bf16 = jnp.bfloat16

D_MODEL = 1024
SEQ = 2048
HEAD_DIM = 64
MEM_HEADS = 4
MEM_WIDTH = MEM_HEADS * HEAD_DIM
TOK_WIDTH = D_MODEL - MEM_WIDTH
HYENA_CH = TOK_WIDTH
HYENA_EMB = 33
FILTER_W = 64
SWA_HEADS = TOK_WIDTH // HEAD_DIM
SWA_KV_HEADS = SWA_HEADS // 3
SWA_GROUP = SWA_HEADS // SWA_KV_HEADS
SWA_IN = (SWA_HEADS + 2 * SWA_KV_HEADS) * HEAD_DIM
WINDOW = 128
ROPE_THETA = 10000.0
LN_EPS = 1e-5
DEPTH = 2
ALPHA = (2 * DEPTH) ** 0.25
NEG_INF = -1e30
Q_SCALE = HEAD_DIM ** -0.5

LANES = 128
MXU_DIM = 256
VMEM_LIMIT_BYTES = 56 * 1024 * 1024

FFT_N = 2 * SEQ
FFT_N2 = LANES
FFT_N1 = FFT_N // FFT_N2
FFT_R = SEQ // FFT_N2
FFT_NL = FFT_N2 * HYENA_CH
FFT_LB = 32 * HYENA_CH


def _cparams(semantics):
    return pltpu.CompilerParams(dimension_semantics=semantics,
                                vmem_limit_bytes=VMEM_LIMIT_BYTES)


@functools.cache
def _fft_tables():
    n1 = np.arange(FFT_R)[None, :]
    k1 = np.arange(FFT_N1)[:, None]
    ang = 2.0 * np.pi * k1 * n1 / FFT_N1
    c, s = np.cos(ang), np.sin(ang)
    f_pair = np.block([[c, s], [-s, c]])
    f_real = np.concatenate([c, -s], axis=0)
    f_inv = np.block([[c.T, -s.T], [s.T, c.T]]) / FFT_N
    n2 = np.arange(FFT_N2)[None, None, :]
    k2 = np.arange(FFT_N2)[None, :, None]
    kk1 = np.arange(FFT_N1)[:, None, None]
    ang2 = 2.0 * np.pi * (n2 * k2 / FFT_N2 + n2 * kk1 / FFT_N)
    gr, gi = np.cos(ang2), -np.sin(ang2)
    g_fwd = np.concatenate([np.concatenate([gr, -gi], axis=2),
                            np.concatenate([gi, gr], axis=2)], axis=1)
    g_inv = np.transpose(g_fwd, (0, 2, 1))
    as_f32 = lambda a: np.asarray(a, dtype=np.float32)
    return dict(f_pair=as_f32(f_pair), f_real=as_f32(f_real), f_inv=as_f32(f_inv),
                g_fwd=as_f32(g_fwd), g_inv=as_f32(g_inv))


@functools.cache
def _filter_tables():
    L = SEQ
    t = np.linspace(0.0, 1.0, L)[:, None]
    bands = (HYENA_EMB - 1) // 2
    w = 2.0 * np.pi * np.arange(L)[:, None] / L
    fr = np.linspace(1e-4, bands - 1, bands)[None, :]
    z = np.concatenate([t, np.cos(fr * w), -np.sin(fr * w)], axis=-1)
    zz = np.zeros((L, LANES), np.float32)
    zz[:, :HYENA_EMB] = z
    min_decay = math.log(1e-2) / 1.5
    max_decay = math.log(1e-2) / 0.3
    deltas = np.abs(np.linspace(min_decay, max_decay, HYENA_CH))[None, :]
    return zz, np.asarray(deltas, dtype=np.float32)


@functools.cache
def _rope_tables():
    inv = ROPE_THETA ** (-np.arange(0, HEAD_DIM, 2) / HEAD_DIM)
    ang = np.arange(SEQ)[:, None] * inv[None, :]
    lane = np.arange(LANES)
    cos = np.cos(ang)[:, lane % (HEAD_DIM // 2)]
    sin = np.sin(ang)[:, lane % (HEAD_DIM // 2)]
    sign = np.where((lane % HEAD_DIM) < HEAD_DIM // 2, -1.0, 1.0)[None, :]
    return np.asarray(cos, dtype=np.float32), np.asarray(sin * sign, dtype=np.float32)


def _dot(a, b):
    return jnp.dot(a, b, preferred_element_type=f32)


def _layer_norm(r, g, b):
    mu = jnp.mean(r, axis=-1, keepdims=True)
    d = r - mu
    var = jnp.mean(d * d, axis=-1, keepdims=True)
    return d * lax.rsqrt(var + LN_EPS) * g + b


def _dwconv3(h, cw, cb):
    n = h.shape[0]
    row = lax.broadcasted_iota(jnp.int32, h.shape, 0)
    up = jnp.where(row == 0, 0.0, pltpu.roll(h, 1, 0))
    dn = jnp.where(row == n - 1, 0.0, pltpu.roll(h, n - 1, 0))
    return up * cw[0:1, :] + h * cw[1:2, :] + dn * cw[2:3, :] + cb


def _memkv_body(mem_ref, w_ref, kt_ref, v_ref):
    kv = _dot(mem_ref[0].astype(bf16), w_ref[...].astype(bf16))
    kt_ref[0] = kv[:, :MEM_WIDTH].T.astype(bf16)
    v_ref[0] = kv[:, MEM_WIDTH:].astype(bf16)


def _memkv(mem, w):
    B, M, D = mem.shape
    return pl.pallas_call(
        _memkv_body,
        grid=(B,),
        in_specs=[pl.BlockSpec((1, M, D), lambda b: (b, 0, 0)),
                  pl.BlockSpec((D, 2 * MEM_WIDTH), lambda b: (0, 0))],
        out_specs=[pl.BlockSpec((1, MEM_WIDTH, M), lambda b: (b, 0, 0)),
                   pl.BlockSpec((1, M, MEM_WIDTH), lambda b: (b, 0, 0))],
        out_shape=[jax.ShapeDtypeStruct((B, MEM_WIDTH, M), bf16),
                   jax.ShapeDtypeStruct((B, M, MEM_WIDTH), bf16)],
        compiler_params=_cparams(("arbitrary",)),
        name="memkv",
    )(mem, w)


FILT_ROWS = 512


def _filter_body(zz_ref, w1, b1, f1, w2, b2, f2, w3, b3, f3, wo_ref, dl_ref, out_ref):
    hp = lax.Precision.HIGHEST
    z = zz_ref[...]
    h = jnp.sin(f1[...] * (jnp.dot(z, w1[...], precision=hp, preferred_element_type=f32) + b1[...]))
    h = jnp.sin(f2[...] * (jnp.dot(h, w2[...], precision=hp, preferred_element_type=f32) + b2[...]))
    h = jnp.sin(f3[...] * (jnp.dot(h, w3[...], precision=hp, preferred_element_type=f32) + b3[...]))
    o = jnp.dot(h, wo_ref[...], precision=hp, preferred_element_type=f32)
    decay = jnp.exp(-z[:, 0:1] * dl_ref[...])
    hf = o[:, :HYENA_CH] * decay
    hb = o[:, HYENA_CH:] * decay
    row = pl.program_id(0) * FILT_ROWS + lax.broadcasted_iota(jnp.int32, hf.shape, 0)
    lag0 = row == 0
    out_ref[0] = hf + jnp.where(lag0, hb, 0.0)
    out_ref[1] = jnp.where(lag0, 0.0, hb)


def _hyena_filters(w1, b1, f1, w2, b2, f2, w3, b3, f3, w_out):
    zz, deltas = _filter_tables()
    w1p = jnp.zeros((LANES, FILTER_W), f32).at[:HYENA_EMB].set(w1)
    row = lambda a: a.reshape(1, -1)
    full = lambda a: pl.BlockSpec(a.shape, lambda i: (0,) * a.ndim)
    args = (w1p, row(b1), row(f1), w2, row(b2), row(f2), w3, row(b3), row(f3), w_out, deltas)
    return pl.pallas_call(
        _filter_body,
        grid=(SEQ // FILT_ROWS,),
        in_specs=[pl.BlockSpec((FILT_ROWS, LANES), lambda i: (i, 0))] + [full(a) for a in args],
        out_specs=pl.BlockSpec((2, FILT_ROWS, HYENA_CH), lambda i: (0, i, 0)),
        out_shape=jax.ShapeDtypeStruct((2, SEQ, HYENA_CH), f32),
        compiler_params=_cparams(("arbitrary",)),
        name="hyena_filter",
    )(zz, *args)


def _fwda_body(f_ref, x_ref, o_ref):
    x = x_ref[...]
    x = x.reshape(-1, x.shape[-1]).astype(bf16)
    o_ref[...] = _dot(f_ref[...].astype(bf16), x).astype(o_ref.dtype)


def _fwd_slow_pair(z, f_pair):
    B = z.shape[0]
    P = B // 2
    zv = z.reshape(2, P, FFT_R, FFT_NL)
    nj = FFT_NL // FFT_LB
    return pl.pallas_call(
        _fwda_body,
        grid=(P, nj),
        in_specs=[pl.BlockSpec(f_pair.shape, lambda p, j: (0, 0)),
                  pl.BlockSpec((2, None, FFT_R, FFT_LB), lambda p, j: (0, p, 0, j))],
        out_specs=pl.BlockSpec((None, 2 * FFT_N1, FFT_LB), lambda p, j: (p, 0, j)),
        out_shape=jax.ShapeDtypeStruct((P, 2 * FFT_N1, FFT_NL), bf16),
        compiler_params=_cparams(("arbitrary", "arbitrary")),
        name="fft_fwd_slow",
    )(f_pair, zv)


def _fwd_slow_real(h, f_real):
    G = h.shape[0]
    hv = h.reshape(G, FFT_R, FFT_NL)
    nj = FFT_NL // FFT_LB
    return pl.pallas_call(
        _fwda_body,
        grid=(G, nj),
        in_specs=[pl.BlockSpec(f_real.shape, lambda g, j: (0, 0)),
                  pl.BlockSpec((None, FFT_R, FFT_LB), lambda g, j: (g, 0, j))],
        out_specs=pl.BlockSpec((None, 2 * FFT_N1, FFT_LB), lambda g, j: (g, 0, j)),
        out_shape=jax.ShapeDtypeStruct((G, 2 * FFT_N1, FFT_NL), bf16),
        compiler_params=_cparams(("arbitrary", "arbitrary")),
        name="fft_fwd_slow_filter",
    )(f_real, hv)


def _kf_body(g_ref, a_ref, kf_ref):
    g = g_ref[...].astype(bf16)
    zf =_dot(g, a_ref[0].reshape(2 * FFT_N2, HYENA_CH))
    zb = _dot(g, a_ref[1].reshape(2 * FFT_N2, HYENA_CH))
    kf_ref[0:FFT_N2, :] = zf[:FFT_N2] + zb[:FFT_N2]
    kf_ref[FFT_N2:, :] = zf[FFT_N2:] - zb[FFT_N2:]


def _filter_spectrum(a_f, g_fwd):
    av = a_f.reshape(2, 2, FFT_N1, FFT_N2, HYENA_CH)
    return pl.pallas_call(
        _kf_body,
        grid=(FFT_N1,),
        in_specs=[pl.BlockSpec((None, 2 * FFT_N2, 2 * FFT_N2), lambda k: (k, 0, 0)),
                  pl.BlockSpec((2, 2, None, FFT_N2, HYENA_CH), lambda k: (0, 0, k, 0, 0))],
        out_specs=pl.BlockSpec((None, 2 * FFT_N2, HYENA_CH), lambda k: (k, 0, 0)),
        out_shape=jax.ShapeDtypeStruct((FFT_N1, 2 * FFT_N2, HYENA_CH), f32),
        compiler_params=_cparams(("arbitrary",)),
        name="fft_filter_spectrum",
    )(g_fwd, av)


def _midb_body(gf_ref, gi_ref, kf_ref, a_ref, u_ref):
    a = a_ref[...].reshape(2 * FFT_N2, HYENA_CH)
    z = _dot(gf_ref[...].astype(bf16), a)
    zr, zi = z[:FFT_N2], z[FFT_N2:]
    kr, ki = kf_ref[0:FFT_N2, :], kf_ref[FFT_N2:, :]
    p = jnp.concatenate([zr * kr - zi * ki, zr * ki + zi * kr], axis=0).astype(bf16)
    u = _dot(gi_ref[...].astype(bf16), p)
    u_ref[...] = u.astype(bf16).reshape(2, FFT_N2, HYENA_CH)


def _fast_stage(a, kf, g_fwd, g_inv):
    P = a.shape[0]
    av = a.reshape(P, 2, FFT_N1, FFT_N2, HYENA_CH)
    gspec = pl.BlockSpec((None, 2 * FFT_N2, 2 * FFT_N2), lambda k, p: (k, 0, 0))
    dspec = pl.BlockSpec((None, 2, None, FFT_N2, HYENA_CH), lambda k, p: (p, 0, k, 0, 0))
    u = pl.pallas_call(
        _midb_body,
        grid=(FFT_N1, P),
        in_specs=[gspec, gspec,
                  pl.BlockSpec((None, 2 * FFT_N2, HYENA_CH), lambda k, p: (k, 0, 0)),
                  dspec],
        out_specs=dspec,
        out_shape=jax.ShapeDtypeStruct(av.shape, bf16),
        compiler_params=_cparams(("arbitrary", "arbitrary")),
        name="fft_fast_stage",
    )(g_fwd, g_inv, kf, av)
    return u.reshape(P, 2 * FFT_N1, FFT_NL)


def _inva_body(f_ref, u_ref, z_ref, x0_ref, d_ref, y_ref):
    y = _dot(f_ref[...].astype(bf16), u_ref[...])
    y = y.reshape(2, FFT_R, y.shape[-1])
    y_ref[...] = ((y + z_ref[...] * d_ref[...]) * x0_ref[...]).astype(y_ref.dtype)


def _inv_slow_gate(u, z, x0, d_bias, f_inv):
    B, L, C = z.shape
    P = B // 2
    nj = FFT_NL // FFT_LB
    view = lambda a: a.reshape(2, P, FFT_R, FFT_NL)
    d_tiled = jnp.tile(d_bias.astype(f32), FFT_LB // C).reshape(1, FFT_LB)
    dspec = pl.BlockSpec((2, None, FFT_R, FFT_LB), lambda p, j: (0, p, 0, j))
    y = pl.pallas_call(
        _inva_body,
        grid=(P, nj),
        in_specs=[pl.BlockSpec(f_inv.shape, lambda p, j: (0, 0)),
                  pl.BlockSpec((None, 2 * FFT_N1, FFT_LB), lambda p, j: (p, 0, j)),
                  dspec, dspec,
                  pl.BlockSpec((1, FFT_LB), lambda p, j: (0, 0))],
        out_specs=dspec,
        out_shape=jax.ShapeDtypeStruct((2, P, FFT_R, FFT_NL), bf16),
        compiler_params=_cparams(("arbitrary", "arbitrary")),
        name="fft_inv_slow_gate",
    )(f_inv, u, view(z), view(x0), d_tiled)
    return y.reshape(B, L, C)


PROJ0_COLS = 256


def _proj0_body(x_ref, w0, w1, w2, cw0, cw1, cw2, cb0, cb1, cb2, z_ref, x0_ref, xb_ref):
    @pl.when(pl.program_id(1) == 0)
    def _():
        xb_ref[...] = x_ref[0].astype(bf16)

    xb = xb_ref[...]

    def branch(w, cw, cb):
        return _dwconv3(_dot(xb, w[...].astype(bf16)), cw[...], cb[...])

    x0 = branch(w0, cw0, cb0)
    x1 = branch(w1, cw1, cb1)
    v = branch(w2, cw2, cb2)
    z_ref[0] = v * x1
    x0_ref[0] = x0


def _proj0(x, w_in, conv_w, conv_b):
    B, L, D = x.shape
    nc = HYENA_CH // PROJ0_COLS
    cb = conv_b.reshape(1, -1)
    wspec = lambda k: pl.BlockSpec((D, PROJ0_COLS), lambda b, j, k=k: (0, k * nc + j))
    cwspec = lambda k: pl.BlockSpec((3, PROJ0_COLS), lambda b, j, k=k: (0, k * nc + j))
    cbspec = lambda k: pl.BlockSpec((1, PROJ0_COLS), lambda b, j, k=k: (0, k * nc + j))
    ospec = pl.BlockSpec((1, L, PROJ0_COLS), lambda b, j: (b, 0, j))
    return pl.pallas_call(
        _proj0_body,
        grid=(B, nc),
        in_specs=[pl.BlockSpec((1, L, D), lambda b, j: (b, 0, 0)),
                  wspec(0), wspec(1), wspec(2),
                  cwspec(0), cwspec(1), cwspec(2),
                  cbspec(0), cbspec(1), cbspec(2)],
        out_specs=[ospec, ospec],
        out_shape=[jax.ShapeDtypeStruct((B, L, HYENA_CH), f32)] * 2,
        scratch_shapes=[pltpu.VMEM((L, D), bf16)],
        compiler_params=_cparams(("arbitrary", "arbitrary")),
        name="proj0_conv_gate",
    )(x, w_in, w_in, w_in, conv_w, conv_w, conv_w, cb, cb, cb)


OUT_ROWS = 512


def _outproj_body(x_ref, yt_ref, kt_ref, v_ref, wq_ref, wo_ref, g_ref, b_ref, o_ref, wqb, wob):
    @pl.when((pl.program_id(0) == 0) & (pl.program_id(1) == 0))
    def _():
        wqb[...] = wq_ref[...].astype(bf16)
        wob[...] = wo_ref[...].astype(bf16)

    x = x_ref[0]
    mq = (_dot(x.astype(bf16), wqb[...]) * Q_SCALE).astype(bf16)
    kt = kt_ref[0]
    v = v_ref[0]
    head = lax.broadcasted_iota(jnp.int32, (1, MEM_WIDTH), 1) // HEAD_DIM
    ymem = jnp.zeros(mq.shape, f32)
    for h in range(MEM_HEADS):
        mine = head == h
        s = _dot(jnp.where(mine, mq, jnp.zeros_like(mq)), kt)
        p = jnp.exp(s - jnp.max(s, axis=-1, keepdims=True))
        denom = jnp.sum(p, axis=-1, keepdims=True)
        ymem = jnp.where(mine, _dot(p.astype(bf16), v) / denom, ymem)
    ycat = jnp.concatenate([yt_ref[0], ymem.astype(bf16)], axis=1)
    y = _dot(ycat, wob[...])
    o_ref[0] = _layer_norm(ALPHA * x + y, g_ref[...], b_ref[...])


def _outproj(x, y_tok, mem_kt, mem_v, w_in, w_out, ln_g, ln_b):
    B, L, D = x.shape
    M = mem_v.shape[1]
    q_block = w_in.shape[1] // MEM_WIDTH - 1
    return pl.pallas_call(
        _outproj_body,
        grid=(B, L // OUT_ROWS),
        in_specs=[pl.BlockSpec((1, OUT_ROWS, D), lambda b, i: (b, i, 0)),
                  pl.BlockSpec((1, OUT_ROWS, TOK_WIDTH), lambda b, i: (b, i, 0)),
                  pl.BlockSpec((1, MEM_WIDTH, M), lambda b, i: (b, 0, 0)),
                  pl.BlockSpec((1, M, MEM_WIDTH), lambda b, i: (b, 0, 0)),
                  pl.BlockSpec((D, MEM_WIDTH), lambda b, i: (0, q_block)),
                  pl.BlockSpec((D, D), lambda b, i: (0, 0)),
                  pl.BlockSpec((1, D), lambda b, i: (0, 0)),
                  pl.BlockSpec((1, D), lambda b, i: (0, 0))],
        out_specs=pl.BlockSpec((1, OUT_ROWS, D), lambda b, i: (b, i, 0)),
        out_shape=jax.ShapeDtypeStruct((B, L, D), f32),
        scratch_shapes=[pltpu.VMEM((D, MEM_WIDTH), bf16), pltpu.VMEM((D, D), bf16)],
        compiler_params=_cparams(("arbitrary", "arbitrary")),
        name="memattn_outproj_ln",
    )(x, y_tok, mem_kt, mem_v, w_in, w_out, ln_g.reshape(1, D), ln_b.reshape(1, D))


FFN_COLS = 256
FFN_ROW_CHUNK = 512


def _ffn_body(x_ref, wa, wg, cwa, cwg, cba, cbg, wd, g_ref, b_ref, o_ref, xb_ref, act_ref):
    j = pl.program_id(1)
    n_rows = xb_ref.shape[0]

    @pl.when(j == 0)
    def _():
        for r in range(0, n_rows, FFN_ROW_CHUNK):
            rows = pl.ds(r, FFN_ROW_CHUNK)
            xr = x_ref[0, rows, :]
            xb_ref[rows, :] = xr.astype(bf16)
            o_ref[0, rows, :] = ALPHA * xr

    xb = xb_ref[...]
    a = _dwconv3(_dot(xb, wa[...].astype(bf16)), cwa[...], cba[...])
    g = _dwconv3(_dot(xb, wg[...].astype(bf16)), cwg[...], cbg[...])
    act_ref[...] = (g * (1.0 / (1.0 + jnp.exp(-g))) * a).astype(bf16)
    wdb = wd[...].astype(bf16)
    for r in range(0, n_rows, FFN_ROW_CHUNK):
        rows = pl.ds(r, FFN_ROW_CHUNK)
        o_ref[0, rows, :] += _dot(act_ref[rows, :], wdb)

    @pl.when(j == pl.num_programs(1) - 1)
    def _():
        for r in range(0, n_rows, FFN_ROW_CHUNK):
            rows = pl.ds(r, FFN_ROW_CHUNK)
            o_ref[0, rows, :] = _layer_norm(o_ref[0, rows, :], g_ref[...], b_ref[...])


def _ffn(x, w_up, conv_w, conv_b, w_down, ln_g, ln_b):
    B, L, D = x.shape
    d_ff = w_down.shape[0]
    nc = d_ff // FFN_COLS
    cb = conv_b.reshape(1, -1)
    once = pl.Buffered(1)
    return pl.pallas_call(
        _ffn_body,
        grid=(B, nc),
        in_specs=[pl.BlockSpec((1, L, D), lambda b, j: (b, 0, 0), pipeline_mode=once),
                  pl.BlockSpec((D, FFN_COLS), lambda b, j: (0, j)),
                  pl.BlockSpec((D, FFN_COLS), lambda b, j: (0, nc + j)),
                  pl.BlockSpec((3, FFN_COLS), lambda b, j: (0, j)),
                  pl.BlockSpec((3, FFN_COLS), lambda b, j: (0, nc + j)),
                  pl.BlockSpec((1, FFN_COLS), lambda b, j: (0, j)),
                  pl.BlockSpec((1, FFN_COLS), lambda b, j: (0, nc + j)),
                  pl.BlockSpec((FFN_COLS, D), lambda b, j: (j, 0)),
                  pl.BlockSpec((1, D), lambda b, j: (0, 0)),
                  pl.BlockSpec((1, D), lambda b, j: (0, 0))],
        out_specs=pl.BlockSpec((1, L, D), lambda b, j: (b, 0, 0)),
        out_shape=jax.ShapeDtypeStruct((B, L, D), f32),
        scratch_shapes=[pltpu.VMEM((L, D), bf16), pltpu.VMEM((L, FFN_COLS), bf16)],
        compiler_params=_cparams(("arbitrary", "arbitrary")),
        name="conv_glu_ffn_ln",
    )(x, w_up, w_up, conv_w, conv_w, cb, cb, w_down, ln_g.reshape(1, D), ln_b.reshape(1, D))


PROJ1_ROWS = 512
QK_WIDTH = (SWA_HEADS + SWA_KV_HEADS) * HEAD_DIM


def _proj1_body(x_ref, w_ref, cos_ref, sin_ref, q_ref, k_ref, v_ref, wb):
    @pl.when((pl.program_id(0) == 0) & (pl.program_id(1) == 0))
    def _():
        wb[...] = w_ref[...].astype(bf16)

    h = _dot(x_ref[0].astype(bf16), wb[...])
    qk = h[:, :QK_WIDTH]
    reps = QK_WIDTH // LANES
    cos = jnp.concatenate([cos_ref[...]] * reps, axis=1)
    sin = jnp.concatenate([sin_ref[...]] * reps, axis=1)
    lane = lax.broadcasted_iota(jnp.int32, qk.shape, 1)
    lower = (lane % HEAD_DIM) < HEAD_DIM // 2
    half = HEAD_DIM // 2
    partner = jnp.where(lower, pltpu.roll(qk, QK_WIDTH - half, 1), pltpu.roll(qk, half, 1))
    r = qk * cos + partner * sin
    for hh in range(SWA_HEADS):
        q_ref[0, hh] = (r[:, hh * HEAD_DIM:(hh + 1) * HEAD_DIM] * Q_SCALE).astype(bf16)
    for g in range(SWA_KV_HEADS):
        lo = (SWA_HEADS + g) * HEAD_DIM
        k_ref[0, g] = r[:, lo:lo + HEAD_DIM].astype(bf16)
        lo = QK_WIDTH + g * HEAD_DIM
        v_ref[0, g] = h[:, lo:lo + HEAD_DIM].astype(bf16)


def _proj1(x, w_in):
    B, L, D = x.shape
    cos, sin = _rope_tables()
    hm = lambda n: pl.BlockSpec((1, n, PROJ1_ROWS, HEAD_DIM), lambda b, i: (b, 0, i, 0))
    return pl.pallas_call(
        _proj1_body,
        grid=(B, L // PROJ1_ROWS),
        in_specs=[pl.BlockSpec((1, PROJ1_ROWS, D), lambda b, i: (b, i, 0)),
                  pl.BlockSpec((D, SWA_IN), lambda b, i: (0, 0)),
                  pl.BlockSpec((PROJ1_ROWS, LANES), lambda b, i: (i, 0)),
                  pl.BlockSpec((PROJ1_ROWS, LANES), lambda b, i: (i, 0))],
        out_specs=[hm(SWA_HEADS), hm(SWA_KV_HEADS), hm(SWA_KV_HEADS)],
        out_shape=[jax.ShapeDtypeStruct((B, SWA_HEADS, L, HEAD_DIM), bf16),
                   jax.ShapeDtypeStruct((B, SWA_KV_HEADS, L, HEAD_DIM), bf16),
                   jax.ShapeDtypeStruct((B, SWA_KV_HEADS, L, HEAD_DIM), bf16)],
        scratch_shapes=[pltpu.VMEM((D, SWA_IN), bf16)],
        compiler_params=_cparams(("arbitrary", "arbitrary")),
        name="proj1_rope",
    )(x, w_in, cos, sin)


SWA_Q_ROWS = 256
SWA_SUB = SWA_Q_ROWS // WINDOW


def _swa_body(sink_ref, q_ref, km, kp, kn, vm, vp, vn, o_ref):
    i = pl.program_id(1)
    n_blocks = pl.num_programs(1) * SWA_SUB
    qi = lax.broadcasted_iota(jnp.int32, (WINDOW, 3 * WINDOW), 0)
    ki = lax.broadcasted_iota(jnp.int32, (WINDOW, 3 * WINDOW), 1)
    rel = ki - WINDOW - qi
    in_band = jnp.abs(rel) <= WINDOW
    kwin = [jnp.concatenate([kp[0, g], km[0, g], kn[0, g]], axis=0) for g in range(SWA_KV_HEADS)]
    vwin = [jnp.concatenate([vp[0, g], vm[0, g], vn[0, g]], axis=0) for g in range(SWA_KV_HEADS)]
    for s in range(SWA_SUB):
        blk = i * SWA_SUB + s
        k_lo = jnp.where(blk > 0, 0, WINDOW)
        k_hi = jnp.where(blk < n_blocks - 1, 3 * WINDOW, 2 * WINDOW)
        valid = in_band & (ki >= k_lo) & (ki < k_hi)
        outs = []
        for g in range(SWA_KV_HEADS):
            kw = kwin[g][s * WINDOW:(s + 3) * WINDOW]
            vw = vwin[g][s * WINDOW:(s + 3) * WINDOW]
            for j in range(SWA_GROUP):
                h = g * SWA_GROUP + j
                qh = q_ref[0, h, s * WINDOW:(s + 1) * WINDOW, :]
                sc = lax.dot_general(qh, kw, (((1,), (1,)), ((), ())), preferred_element_type=f32)
                sc = jnp.where(valid, sc, NEG_INF)
                sk = sink_ref[h]
                m = jnp.maximum(jnp.max(sc, axis=-1, keepdims=True), sk)
                p = jnp.exp(sc - m)
                denom = jnp.sum(p, axis=-1, keepdims=True) + jnp.exp(sk - m)
                outs.append(_dot(p.astype(bf16), vw) / denom)
        o_ref[0, s * WINDOW:(s + 1) * WINDOW, :] = jnp.concatenate(outs, axis=1).astype(o_ref.dtype)


def _swa(q, k, v, sink):
    B, H, L, hd = q.shape
    nb = L // WINDOW
    main = lambda n: pl.BlockSpec((1, n, SWA_Q_ROWS, hd), lambda b, i: (b, 0, i, 0))
    prev = pl.BlockSpec((1, SWA_KV_HEADS, WINDOW, hd),
                        lambda b, i: (b, 0, jnp.maximum(i * SWA_SUB - 1, 0), 0))
    nxt = pl.BlockSpec((1, SWA_KV_HEADS, WINDOW, hd),
                       lambda b, i: (b, 0, jnp.minimum((i + 1) * SWA_SUB, nb - 1), 0))
    return pl.pallas_call(
        _swa_body,
        grid=(B, L // SWA_Q_ROWS),
        in_specs=[pl.BlockSpec(memory_space=pltpu.SMEM),
                  main(H), main(SWA_KV_HEADS), prev, nxt, main(SWA_KV_HEADS), prev, nxt],
        out_specs=pl.BlockSpec((1, SWA_Q_ROWS, H * hd), lambda b, i: (b, i, 0)),
        out_shape=jax.ShapeDtypeStruct((B, L, H * hd), bf16),
        compiler_params=_cparams(("arbitrary", "arbitrary")),
        name="swa_sink",
    )(sink.astype(f32), q, k, k, k, v, v, v)


def kernel(x, mem, w_mem_kv, l0_w_in, l0_conv_w, l0_conv_b, l0_filt_w1, l0_filt_b1, l0_filt_f1, l0_filt_w2, l0_filt_b2, l0_filt_f2, l0_filt_w3, l0_filt_b3, l0_filt_f3, l0_filt_w_out, l0_hyena_d, l0_w_out, l0_ln1_g, l0_ln1_b, l0_ffn_w_up, l0_ffn_conv_w, l0_ffn_conv_b, l0_ffn_w_down, l0_ln2_g, l0_ln2_b, l1_w_in, l1_sink, l1_w_out, l1_ln1_g, l1_ln1_b, l1_ffn_w_up, l1_ffn_conv_w, l1_ffn_conv_b, l1_ffn_w_down, l1_ln2_g, l1_ln2_b):
    assert x.shape[1:] == (SEQ, D_MODEL) and x.shape[0] % 2 == 0
    tabs = _fft_tables()
    mem_kt, mem_v = _memkv(mem, w_mem_kv)

    filt = _hyena_filters(l0_filt_w1, l0_filt_b1, l0_filt_f1, l0_filt_w2, l0_filt_b2, l0_filt_f2,
                          l0_filt_w3, l0_filt_b3, l0_filt_f3, l0_filt_w_out)
    kf = _filter_spectrum(_fwd_slow_real(filt, tabs["f_real"]), tabs["g_fwd"])
    z, x0 = _proj0(x, l0_w_in, l0_conv_w, l0_conv_b)
    a = _fwd_slow_pair(z, tabs["f_pair"])
    u = _fast_stage(a, kf, tabs["g_fwd"], tabs["g_inv"])
    y_tok = _inv_slow_gate(u, z, x0, l0_hyena_d, tabs["f_inv"])
    x = _outproj(x, y_tok, mem_kt, mem_v, l0_w_in, l0_w_out, l0_ln1_g, l0_ln1_b)
    x = _ffn(x, l0_ffn_w_up, l0_ffn_conv_w, l0_ffn_conv_b, l0_ffn_w_down, l0_ln2_g, l0_ln2_b)

    q, k, v = _proj1(x, l1_w_in)
    y_tok = _swa(q, k, v, l1_sink)
    x = _outproj(x, y_tok, mem_kt, mem_v, l1_w_in, l1_w_out, l1_ln1_g, l1_ln1_b)
    x = _ffn(x, l1_ffn_w_up, l1_ffn_conv_w, l1_ffn_conv_b, l1_ffn_w_down, l1_ln2_g, l1_ln2_b)
    return x
```
